```python
import math
import jax, jax.numpy as jnp
from jax import lax
import numpy as np

D_MODEL = 1024
BATCH = 16
SEQ = 2048
DEPTH = 1

N_ATTN_HEADS = 8
HEAD_DIM = 64
D_ATTN = N_ATTN_HEADS * HEAD_DIM
MOBA_BLOCK = 256
MOBA_TOPK = 3
Q_CHUNK = 16
N_BUCKETS = 32
MAX_EXACT = N_BUCKETS // 2
MAX_DISTANCE = 128
D_LRU = D_MODEL // 2
N_LRU_BLOCKS = 8
LRU_BLOCK = D_LRU // N_LRU_BLOCKS
LRU_CONV_W = 4
LRU_C = 8.0
D_MIX = D_ATTN + D_LRU
D_IN_PROJ = 3 * D_ATTN + 2 * D_LRU
D_FF = 2816
FFN_CONV_W = 3
RMS_EPS = 1e-6

kernel_name = "hymba_moba_rglru_convffn_sandwich"


def rms_norm(x, g):
    xf = x.astype(jnp.float32)
    y = xf * lax.rsqrt(jnp.mean(xf * xf, axis=-1, keepdims=True) + RMS_EPS)
    return (y * g.astype(jnp.float32)).astype(x.dtype)


def causal_dwconv(x, w, b):
    width, ch = w.shape
    y = lax.conv_general_dilated(
        x, w[:, None, :].astype(x.dtype), window_strides=(1,), padding=[(width - 1, 0)],
        dimension_numbers=("NWC", "WIO", "NWC"), feature_group_count=ch)
    return y + b.astype(x.dtype)


def t5_bucket(dist):
    n = jnp.maximum(dist, 0)
    nf = jnp.maximum(n, 1).astype(jnp.float32)
    large = MAX_EXACT + (jnp.log(nf / MAX_EXACT) / math.log(MAX_DISTANCE / MAX_EXACT)
                         * (N_BUCKETS - MAX_EXACT)).astype(jnp.int32)
    large = jnp.minimum(large, N_BUCKETS - 1)
    return jnp.where(n < MAX_EXACT, n, large)


def moba_attention(q, k, v, rel_bias):
    B, H, S, dh = q.shape
    nb = -(-S // MOBA_BLOCK)
    pad = nb * MOBA_BLOCK - S
    k_pad = jnp.pad(k, ((0, 0), (0, 0), (0, pad), (0, 0)))
    v_pad = jnp.pad(v, ((0, 0), (0, 0), (0, pad), (0, 0)))
    k_blocks = k_pad.reshape(B, H, nb, MOBA_BLOCK, dh)
    v_blocks = v_pad.reshape(B, H, nb, MOBA_BLOCK, dh)
    k_mean = jnp.mean(k_blocks.astype(jnp.float32), axis=3)
    top = min(MOBA_TOPK, nb)
    scale = HEAD_DIM ** -0.5
    key_off = jnp.arange(MOBA_BLOCK, dtype=jnp.int32)
    b_i = jnp.arange(B)[:, None, None, None]
    h_i = jnp.arange(H)[None, :, None, None]
    bias_T = rel_bias.astype(jnp.float32).T

    def chunk(c):
        q0 = c * Q_CHUNK
        qc = lax.dynamic_slice_in_dim(q, q0, Q_CHUNK, axis=2)
        q_pos = q0 + jnp.arange(Q_CHUNK, dtype=jnp.int32)
        own = q0 // MOBA_BLOCK
        gate = jnp.einsum("bhqd,bhnd->bhqn", qc.astype(jnp.float32), k_mean)
        past = jnp.arange(nb) < own
        gate = jnp.where(past[None, None, None, :], gate, -jnp.inf)
        _, idx = lax.top_k(gate, top)
        valid = idx < own
        k_sel = k_blocks[b_i, h_i, idx]
        v_sel = v_blocks[b_i, h_i, idx]
        s_sel = jnp.einsum("bhqd,bhqjkd->bhqjk", qc, k_sel,
                           preferred_element_type=jnp.float32) * scale
        k_pos_sel = idx[..., None] * MOBA_BLOCK + key_off
        bucket_sel = t5_bucket(q_pos[None, None, :, None, None] - k_pos_sel)
        s_sel = s_sel + bias_T[h_i[..., None], bucket_sel]
        s_sel = jnp.where(valid[..., None], s_sel, -jnp.inf)
        k_own = lax.dynamic_slice_in_dim(k_pad, own * MOBA_BLOCK, MOBA_BLOCK, axis=2)
        v_own = lax.dynamic_slice_in_dim(v_pad, own * MOBA_BLOCK, MOBA_BLOCK, axis=2)
        s_own = jnp.einsum("bhqd,bhkd->bhqk", qc, k_own,
                           preferred_element_type=jnp.float32) * scale
        dist_own = q_pos[:, None] - (own * MOBA_BLOCK + key_off)[None, :]
        s_own = s_own + jnp.transpose(rel_bias.astype(jnp.float32)[t5_bucket(dist_own)], (2, 0, 1))[None]
        s_own = jnp.where((dist_own >= 0)[None, None], s_own, -jnp.inf)
        logits = jnp.concatenate([s_sel.reshape(B, H, Q_CHUNK, top * MOBA_BLOCK), s_own], axis=-1)
        p = jax.nn.softmax(logits, axis=-1).astype(v.dtype)
        p_sel = p[..., :top * MOBA_BLOCK].reshape(B, H, Q_CHUNK, top, MOBA_BLOCK)
        p_own = p[..., top * MOBA_BLOCK:]
        return (jnp.einsum("bhqjk,bhqjkd->bhqd", p_sel, v_sel)
                + jnp.einsum("bhqk,bhkd->bhqd", p_own, v_own))

    outs = lax.map(chunk, jnp.arange(S // Q_CHUNK))
    return jnp.transpose(outs, (1, 2, 0, 3, 4)).reshape(B, H, S, dh)


def rg_lru(x, w_r, b_r, w_i, b_i, lam):
    B, S, _ = x.shape
    xf = x.astype(jnp.float32)
    xb = xf.reshape(B, S, N_LRU_BLOCKS, LRU_BLOCK)
    r = jax.nn.sigmoid(jnp.einsum("bsnc,ncd->bsnd", xb, w_r.astype(jnp.float32)).reshape(B, S, D_LRU)
                       + b_r.astype(jnp.float32))
    i = jax.nn.sigmoid(jnp.einsum("bsnc,ncd->bsnd", xb, w_i.astype(jnp.float32)).reshape(B, S, D_LRU)
                       + b_i.astype(jnp.float32))
    log_a = -LRU_C * r * jax.nn.softplus(-lam.astype(jnp.float32))
    a = jnp.exp(log_a)
    u = jnp.sqrt(-jnp.expm1(2.0 * log_a)) * (i * xf)

    def combine(left, right):
        a1, b1 = left
        a2, b2 = right
        return a1 * a2, a2 * b1 + b2

    _, h = lax.associative_scan(combine, (a, u), axis=1)
    return h.astype(x.dtype)


def setup_inputs(seed: int = 0) -> dict:
    key = jax.random.key(seed)
    ks = jax.random.split(key, 24)
    nrm = lambda k, shape, s: jax.random.normal(k, shape, jnp.float32) * s
    u_a = jax.random.uniform(ks[10], (DEPTH, D_LRU), jnp.float32, 0.9, 0.999)
    base = u_a ** (1.0 / LRU_C)
    lam = jnp.log(base) - jnp.log1p(-base)
    return {
        "x": nrm(ks[0], (BATCH, SEQ, D_MODEL), 1.0),
        "g_pre_mix": 1.0 + nrm(ks[1], (DEPTH, D_MODEL), 0.05),
        "w_in": nrm(ks[2], (DEPTH, D_MODEL, D_IN_PROJ), D_MODEL ** -0.5),
        "rel_bias": nrm(ks[3], (N_BUCKETS, N_ATTN_HEADS), 0.3),
        "w_conv_lru": nrm(ks[4], (DEPTH, LRU_CONV_W, D_LRU), LRU_CONV_W ** -0.5),
        "b_conv_lru": nrm(ks[5], (DEPTH, D_LRU), 0.02),
        "w_r": nrm(ks[6], (DEPTH, N_LRU_BLOCKS, LRU_BLOCK, LRU_BLOCK), LRU_BLOCK ** -0.5),
        "b_r": nrm(ks[7], (DEPTH, D_LRU), 0.02),
        "w_i": nrm(ks[8], (DEPTH, N_LRU_BLOCKS, LRU_BLOCK, LRU_BLOCK), LRU_BLOCK ** -0.5),
        "b_i": nrm(ks[9], (DEPTH, D_LRU), 0.02),
        "lam": lam,
        "w_out": nrm(ks[11], (DEPTH, D_MIX, D_MODEL), D_MIX ** -0.5),
        "g_post_mix": 1.0 + nrm(ks[12], (DEPTH, D_MODEL), 0.05),
        "g_pre_ffn": 1.0 + nrm(ks[13], (DEPTH, D_MODEL), 0.05),
        "w_up": nrm(ks[14], (DEPTH, D_MODEL, 2 * D_FF), D_MODEL ** -0.5),
        "w_conv_ffn": nrm(ks[15], (DEPTH, FFN_CONV_W, 2 * D_FF), FFN_CONV_W ** -0.5),
        "b_conv_ffn": nrm(ks[16], (DEPTH, 2 * D_FF), 0.02),
        "w_down": nrm(ks[17], (DEPTH, D_FF, D_MODEL), D_FF ** -0.5),
        "g_post_ffn": 1.0 + nrm(ks[18], (DEPTH, D_MODEL), 0.05),
    }


def reference(x, g_pre_mix, w_in, rel_bias, w_conv_lru, b_conv_lru, w_r, b_r, w_i, b_i, lam,
              w_out, g_post_mix, g_pre_ffn, w_up, w_conv_ffn, b_conv_ffn, w_down, g_post_ffn):
    B, S, _ = x.shape
    for l in range(DEPTH):
        h = rms_norm(x, g_pre_mix[l])
        proj = h @ w_in[l].astype(h.dtype)
        q, k, v, xr, gr = jnp.split(
            proj, [D_ATTN, 2 * D_ATTN, 3 * D_ATTN, 3 * D_ATTN + D_LRU], axis=-1)
        to_heads = lambda t: jnp.transpose(t.reshape(B, S, N_ATTN_HEADS, HEAD_DIM), (0, 2, 1, 3))
        attn = moba_attention(to_heads(q), to_heads(k), to_heads(v), rel_bias)
        attn = jnp.transpose(attn, (0, 2, 1, 3)).reshape(B, S, D_ATTN)
        xr = causal_dwconv(xr, w_conv_lru[l], b_conv_lru[l])
        lru = rg_lru(xr, w_r[l], b_r[l], w_i[l], b_i[l], lam[l]) * jax.nn.gelu(gr, approximate=True)
        mix = jnp.concatenate([attn, lru], axis=-1) @ w_out[l].astype(h.dtype)
        x = x + rms_norm(mix, g_post_mix[l])
        h = rms_norm(x, g_pre_ffn[l])
        up = causal_dwconv(h @ w_up[l].astype(h.dtype), w_conv_ffn[l], b_conv_ffn[l])
        c_u, c_g = jnp.split(up, 2, axis=-1)
        ffn = (jax.nn.gelu(c_g, approximate=True) * c_u) @ w_down[l].astype(h.dtype)
        x = x + rms_norm(ffn, g_post_ffn[l])
    return x
```

```python
import functools
import math

import numpy as np
import jax
import jax.numpy as jnp
from jax import lax
from jax.experimental import pallas as pl
from jax.experimental.pallas import tpu as pltpu

D_MODEL = 1024
N_HEADS = 8
HEAD_DIM = 64
D_ATTN = N_HEADS * HEAD_DIM
BLK = 256
TOPK = 3
N_BUCKETS = 32
MAX_EXACT = N_BUCKETS // 2
MAX_DISTANCE = 128
D_LRU = 512
N_LRU_BLOCKS = 8
LRU_BLOCK = D_LRU // N_LRU_BLOCKS
LRU_CONV_W = 4
LRU_C = 8.0
D_FF = 2816
FFN_CONV_W = 3
RMS_EPS = 1e-6

NEG = -1e30
VMEM_LIMIT = 56 * 1024 * 1024
F32 = jnp.float32
BF16 = jnp.bfloat16


def _rms(x, g):
    ms = jnp.mean(x * x, axis=-1, keepdims=True)
    return (x * lax.rsqrt(ms + RMS_EPS)) * g


def _gelu(x):
    c = math.sqrt(2.0 / math.pi)
    return 0.5 * x * (1.0 + jnp.tanh(c * (x + 0.044715 * (x * x * x))))


def _t5_bucket_np(dist):
    n = np.maximum(dist, 0)
    nf = np.maximum(n, 1).astype(np.float64)
    val = np.log(nf / MAX_EXACT) / math.log(MAX_DISTANCE / MAX_EXACT) * (N_BUCKETS - MAX_EXACT)
    frac = np.abs(val - np.round(val))
    risky = (frac < 1e-3) & (n > MAX_EXACT) & (val < N_BUCKETS - MAX_EXACT - 0.5)
    assert not risky.any()
    large = np.minimum(MAX_EXACT + np.floor(val + 1e-6).astype(np.int64), N_BUCKETS - 1)
    return np.where(n < MAX_EXACT, n, large).astype(np.int32)


def _bucket_maps():
    k = np.arange(BLK)[:, None]
    q = np.arange(BLK)[None, :]
    adj = _t5_bucket_np(q + BLK - k)
    own = np.where(q - k >= 0, _t5_bucket_np(q - k), -1)
    return np.concatenate([adj, own], axis=0).astype(np.int32)


def _far_bucket(seq):
    far = _t5_bucket_np(np.arange(BLK + 1, max(seq, BLK + 2)))
    assert (far == far[0]).all()
    return int(far[0])


def _bias_kernel(rb_ref, bmap_ref, out_ref):
    h = pl.program_id(0)
    bmap = bmap_ref[...]
    acc = jnp.full(bmap.shape, NEG, F32)
    for b in range(N_BUCKETS):
        acc = jnp.where(bmap == b, rb_ref[b, h], acc)
    out_ref[0] = acc


def _bias_tiles(rel_bias):
    bmap = jnp.asarray(_bucket_maps())
    return pl.pallas_call(
        _bias_kernel,
        grid=(N_HEADS,),
        in_specs=[
            pl.BlockSpec(memory_space=pltpu.SMEM),
            pl.BlockSpec((2 * BLK, BLK), lambda h: (0, 0)),
        ],
        out_specs=pl.BlockSpec((1, 2 * BLK, BLK), lambda h: (h, 0, 0)),
        out_shape=jax.ShapeDtypeStruct((N_HEADS, 2 * BLK, BLK), F32),
        name="bias_tiles",
    )(rel_bias.astype(F32), bmap)


def _inproj_kernel(x_ref, g_ref, w_ref, q_ref, k_ref, v_ref, xr_ref, gr_ref):
    h = _rms(x_ref[...], g_ref[...]).astype(BF16)

    def proj(lo, hi):
        return jnp.dot(h, w_ref[:, lo:hi], preferred_element_type=F32)

    q_ref[...] = (proj(0, D_ATTN) * (HEAD_DIM ** -0.5)).astype(BF16)
    k_ref[...] = proj(D_ATTN, 2 * D_ATTN).astype(BF16)
    v_ref[...] = proj(2 * D_ATTN, 3 * D_ATTN).astype(BF16)
    xr_ref[...] = proj(3 * D_ATTN, 3 * D_ATTN + D_LRU)
    gr_ref[...] = proj(3 * D_ATTN + D_LRU, 3 * D_ATTN + 2 * D_LRU)


def _inproj(x2, g, w_bf, tm=512):
    n = x2.shape[0]
    d_in = w_bf.shape[1]
    tok = lambda i: (i, 0)
    return pl.pallas_call(
        _inproj_kernel,
        grid=(n // tm,),
        in_specs=[
            pl.BlockSpec((tm, D_MODEL), tok),
            pl.BlockSpec((1, D_MODEL), lambda i: (0, 0)),
            pl.BlockSpec((D_MODEL, d_in), lambda i: (0, 0)),
        ],
        out_specs=[
            pl.BlockSpec((tm, D_ATTN), tok),
            pl.BlockSpec((tm, D_ATTN), tok),
            pl.BlockSpec((tm, D_ATTN), tok),
            pl.BlockSpec((tm, D_LRU), tok),
            pl.BlockSpec((tm, D_LRU), tok),
        ],
        out_shape=[
            jax.ShapeDtypeStruct((n, D_ATTN), BF16),
            jax.ShapeDtypeStruct((n, D_ATTN), BF16),
            jax.ShapeDtypeStruct((n, D_ATTN), BF16),
            jax.ShapeDtypeStruct((n, D_LRU), F32),
            jax.ShapeDtypeStruct((n, D_LRU), F32),
        ],
        compiler_params=pltpu.CompilerParams(
            dimension_semantics=("arbitrary",), vmem_limit_bytes=VMEM_LIMIT),
        name="inproj",
    )(x2, g, w_bf)


def _attn_kernel(rb_ref, q_ref, k_ref, v_ref, bias_ref, o_ref, vt_ref, s_ref, p_ref, *, seq, far_bucket):
    nb = seq // BLK
    hp = pl.program_id(1)
    vt_ref[...] = v_ref[...].T
    lane = lax.broadcasted_iota(jnp.int32, (1, 2 * HEAD_DIM), 1)
    blk_id = lax.broadcasted_iota(jnp.int32, (nb, BLK), 0)

    k_mean = jnp.concatenate(
        [jnp.mean(k_ref[j * BLK:(j + 1) * BLK, :].astype(F32), axis=0, keepdims=True) for j in range(nb)],
        axis=0)
    km_hi = k_mean.astype(BF16)
    km_lo = (k_mean - km_hi.astype(F32)).astype(BF16)
    nt = (((1,), (1,)), ((), ()))

    for i in range(nb):
        q_i = q_ref[i * BLK:(i + 1) * BLK, :]
        outs = []
        for hh in range(2):
            in_head = (lane >= hh * HEAD_DIM) & (lane < (hh + 1) * HEAD_DIM)
            q_h = jnp.where(in_head, q_i, jnp.zeros_like(q_i))
            c_far = rb_ref[far_bucket, 2 * hp + hh]
            gate = (lax.dot_general(km_hi, q_h, nt, preferred_element_type=F32)
                    + lax.dot_general(km_lo, q_h, nt, preferred_element_type=F32))
            rank = jnp.zeros((nb, BLK), jnp.int32)
            for j2 in range(i):
                row = gate[j2:j2 + 1, :]
                beats = (row > gate) | ((row == gate) & (j2 < blk_id))
                rank = rank + beats.astype(jnp.int32)
            sel = rank < TOPK
            row_far = jnp.where(sel, c_far, NEG)
            row_adj = jnp.where(sel, 0.0, NEG)

            m8 = jnp.full((8, BLK), NEG, F32)
            for j in range(i + 1):
                k_j = k_ref[j * BLK:(j + 1) * BLK, :]
                s = lax.dot_general(k_j, q_h, nt, preferred_element_type=F32)
                if j == i:
                    s = s + bias_ref[hh, BLK:2 * BLK, :]
                elif j == i - 1:
                    s = (s + bias_ref[hh, 0:BLK, :]) + row_adj[j:j + 1, :]
                else:
                    s = s + row_far[j:j + 1, :]
                s_ref[j * BLK:(j + 1) * BLK, :] = s
                m8 = jnp.maximum(m8, jnp.max(s.reshape(BLK // 8, 8, BLK), axis=0))
            m = jnp.max(m8, axis=0, keepdims=True)
            l8 = jnp.zeros((8, BLK), F32)
            for j in range(i + 1):
                p = jnp.exp(s_ref[j * BLK:(j + 1) * BLK, :] - m)
                l8 = l8 + jnp.sum(p.reshape(BLK // 8, 8, BLK), axis=0)
                p_ref[j * BLK:(j + 1) * BLK, :] = p.astype(BF16)
            l = jnp.sum(l8, axis=0, keepdims=True)
            nk = (i + 1) * BLK
            o_t = jnp.dot(vt_ref[hh * HEAD_DIM:(hh + 1) * HEAD_DIM, 0:nk], p_ref[0:nk, :],
                          preferred_element_type=F32)
            outs.append(o_t / l)
        o_pair = jnp.concatenate(outs, axis=0)
        o_ref[i * BLK:(i + 1) * BLK, :] = o_pair.T.astype(BF16)


def _attention(q, k, v, bias, rel_bias, batch, seq):
    n = q.shape[0]
    tokpair = lambda b, hp: (b, hp)
    kern = functools.partial(_attn_kernel, seq=seq, far_bucket=_far_bucket(seq))
    return pl.pallas_call(
        kern,
        grid=(batch, N_HEADS // 2),
        in_specs=[
            pl.BlockSpec(memory_space=pltpu.SMEM),
            pl.BlockSpec((seq, 2 * HEAD_DIM), tokpair),
            pl.BlockSpec((seq, 2 * HEAD_DIM), tokpair),
            pl.BlockSpec((seq, 2 * HEAD_DIM), tokpair),
            pl.BlockSpec((2, 2 * BLK, BLK), lambda b, hp: (hp, 0, 0)),
        ],
        out_specs=pl.BlockSpec((seq, 2 * HEAD_DIM), tokpair),
        out_shape=jax.ShapeDtypeStruct((n, D_ATTN), BF16),
        scratch_shapes=[
            pltpu.VMEM((2 * HEAD_DIM, seq), BF16),
            pltpu.VMEM((seq, BLK), F32),
            pltpu.VMEM((seq, BLK), BF16),
        ],
        compiler_params=pltpu.CompilerParams(
            dimension_semantics=("arbitrary", "arbitrary"), vmem_limit_bytes=VMEM_LIMIT),
        name="moba_attn",
    )(rel_bias.astype(F32), q, k, v, bias)


def _lru_kernel(xr_ref, gr_ref, wc_ref, bc_ref, wr_ref, br_ref, wi_ref, bi_ref, lam_ref, o_ref,
                xp_ref, a_ref, u_ref, h_ref, *, seq, chunk):
    pad = 8
    xp_ref[0:pad, :] = jnp.zeros((pad, D_LRU), F32)
    xp_ref[pad:pad + seq, :] = xr_ref[...]
    z = -lam_ref[...]
    softplus = jnp.maximum(z, 0.0) + jnp.log1p(jnp.exp(-jnp.abs(z)))
    c_row = -LRU_C * softplus
    sub = lax.broadcasted_iota(jnp.int32, (chunk // 8, 8, D_LRU), 1)

    carry = jnp.zeros((8, D_LRU), F32)
    for ch in range(seq // chunk):
        t0 = ch * chunk
        xc = bc_ref[...] + sum(
            wc_ref[kk:kk + 1, :] * xp_ref[t0 + pad - (LRU_CONV_W - 1) + kk:t0 + pad - (LRU_CONV_W - 1) + kk + chunk, :]
            for kk in range(LRU_CONV_W))
        xb = xc.astype(BF16)
        r = jax.nn.sigmoid(jnp.dot(xb, wr_ref[...], preferred_element_type=F32) + br_ref[...])
        ig = jax.nn.sigmoid(jnp.dot(xb, wi_ref[...], preferred_element_type=F32) + bi_ref[...])
        log_a = c_row * r
        a = jnp.exp(log_a)
        u = jnp.sqrt(jnp.tanh(-log_a) * (1.0 + a * a)) * (ig * xc)
        a3 = a.reshape(chunk // 8, 8, D_LRU)
        u3 = u.reshape(chunk // 8, 8, D_LRU)
        for d in (1, 2, 4):
            a_s = pltpu.roll(a3, d, 1)
            u_s = pltpu.roll(u3, d, 1)
            live = sub >= d
            u3 = jnp.where(live, a3 * u_s + u3, u3)
            a3 = jnp.where(live, a3 * a_s, a3)
        a_ref[...] = a3.reshape(chunk, D_LRU)
        u_ref[...] = u3.reshape(chunk, D_LRU)

        def group(gi, c):
            rows = pl.ds(pl.multiple_of(gi * 8, 8), 8)
            hg = a_ref[rows, :] * c + u_ref[rows, :]
            h_ref[rows, :] = hg
            return jnp.broadcast_to(hg[7:8, :], (8, D_LRU))

        carry = lax.fori_loop(0, chunk // 8, group, carry)
        o_ref[t0:t0 + chunk, :] = (h_ref[...] * _gelu(gr_ref[t0:t0 + chunk, :])).astype(BF16)


def _lru(xr, gr, wc, bc, wr_bd, br, wi_bd, bi, lam, batch, seq, chunk=256):
    n = xr.shape[0]
    row = lambda b: (b, 0)
    full = lambda b: (0, 0)
    kern = functools.partial(_lru_kernel, seq=seq, chunk=chunk)
    return pl.pallas_call(
        kern,
        grid=(batch,),
        in_specs=[
            pl.BlockSpec((seq, D_LRU), row),
            pl.BlockSpec((seq, D_LRU), row),
            pl.BlockSpec((LRU_CONV_W, D_LRU), full),
            pl.BlockSpec((1, D_LRU), full),
            pl.BlockSpec((D_LRU, D_LRU), full),
            pl.BlockSpec((1, D_LRU), full),
            pl.BlockSpec((D_LRU, D_LRU), full),
            pl.BlockSpec((1, D_LRU), full),
            pl.BlockSpec((1, D_LRU), full),
        ],
        out_specs=pl.BlockSpec((seq, D_LRU), row),
        out_shape=jax.ShapeDtypeStruct((n, D_LRU), BF16),
        scratch_shapes=[
            pltpu.VMEM((seq + 8, D_LRU), F32),
            pltpu.VMEM((chunk, D_LRU), F32),
            pltpu.VMEM((chunk, D_LRU), F32),
            pltpu.VMEM((chunk, D_LRU), F32),
        ],
        compiler_params=pltpu.CompilerParams(
            dimension_semantics=("arbitrary",), vmem_limit_bytes=VMEM_LIMIT),
        name="rg_lru",
    )(xr, gr, wc, bc, wr_bd, br, wi_bd, bi, lam)


def _outproj_kernel(x_ref, a_ref, l_ref, w_ref, g_ref, o_ref):
    mix = (jnp.dot(a_ref[...], w_ref[0:D_ATTN, :], preferred_element_type=F32)
           + jnp.dot(l_ref[...], w_ref[D_ATTN:D_ATTN + D_LRU, :], preferred_element_type=F32))
    o_ref[...] = x_ref[...] + _rms(mix, g_ref[...])


def _outproj(x2, attn, lru, w_bf, g, tm=512):
    n = x2.shape[0]
    tok = lambda i: (i, 0)
    return pl.pallas_call(
        _outproj_kernel,
        grid=(n // tm,),
        in_specs=[
            pl.BlockSpec((tm, D_MODEL), tok),
            pl.BlockSpec((tm, D_ATTN), tok),
            pl.BlockSpec((tm, D_LRU), tok),
            pl.BlockSpec((D_ATTN + D_LRU, D_MODEL), lambda i: (0, 0)),
            pl.BlockSpec((1, D_MODEL), lambda i: (0, 0)),
        ],
        out_specs=pl.BlockSpec((tm, D_MODEL), tok),
        out_shape=jax.ShapeDtypeStruct((n, D_MODEL), F32),
        compiler_params=pltpu.CompilerParams(
            dimension_semantics=("arbitrary",), vmem_limit_bytes=VMEM_LIMIT),
        name="outproj",
    )(x2, attn, lru, w_bf, g)


def _ffn_kernel(x_ref, g1_ref, wu_ref, wc_ref, bc_ref, wd_ref, g2_ref, o_ref,
                halo_ref, us_ref, gs_ref, act_ref, *, tm, fc):
    pad = 8

    @pl.when(pl.program_id(1) == 0)
    def _():
        halo_ref[...] = jnp.zeros(halo_ref.shape, F32)

    x = x_ref[...]
    h = _rms(x, g1_ref[...]).astype(BF16)

    def conv(raw, s_ref, col):
        s_ref[0:pad, :] = halo_ref[:, col:col + fc]
        s_ref[pad:pad + tm, :] = raw
        halo_ref[:, col:col + fc] = raw[tm - pad:tm, :]
        out = bc_ref[:, col:col + fc]
        for kk in range(FFN_CONV_W):
            lo = pad - (FFN_CONV_W - 1) + kk
            out = out + wc_ref[kk:kk + 1, col:col + fc] * s_ref[lo:lo + tm, :]
        return out

    for c in range(D_FF // fc):
        cu = c * fc
        cg = D_FF + c * fc
        u = conv(jnp.dot(h, wu_ref[:, cu:cu + fc], preferred_element_type=F32), us_ref, cu)
        g = conv(jnp.dot(h, wu_ref[:, cg:cg + fc], preferred_element_type=F32), gs_ref, cg)
        act_ref[:, cu:cu + fc] = (_gelu(g) * u).astype(BF16)
    ffn = jnp.dot(act_ref[...], wd_ref[...], preferred_element_type=F32)
    o_ref[...] = x + _rms(ffn, g2_ref[...])


def _ffn(x1, g1, wu_bf, wc, bc, wd_bf, g2, batch, seq, tm=512, fc=256):
    n = x1.shape[0]
    nt = seq // tm
    tok = lambda b, t: (b * nt + t, 0)
    full = lambda b, t: (0, 0)
    kern = functools.partial(_ffn_kernel, tm=tm, fc=fc)
    return pl.pallas_call(
        kern,
        grid=(batch, nt),
        in_specs=[
            pl.BlockSpec((tm, D_MODEL), tok),
            pl.BlockSpec((1, D_MODEL), full),
            pl.BlockSpec((D_MODEL, 2 * D_FF), full, pipeline_mode=pl.Buffered(1)),
            pl.BlockSpec((FFN_CONV_W, 2 * D_FF), full),
            pl.BlockSpec((1, 2 * D_FF), full),
            pl.BlockSpec((D_FF, D_MODEL), full, pipeline_mode=pl.Buffered(1)),
            pl.BlockSpec((1, D_MODEL), full),
        ],
        out_specs=pl.BlockSpec((tm, D_MODEL), tok),
        out_shape=jax.ShapeDtypeStruct((n, D_MODEL), F32),
        scratch_shapes=[
            pltpu.VMEM((8, 2 * D_FF), F32),
            pltpu.VMEM((tm + 8, fc), F32),
            pltpu.VMEM((tm + 8, fc), F32),
            pltpu.VMEM((tm, D_FF), BF16),
        ],
        compiler_params=pltpu.CompilerParams(
            dimension_semantics=("arbitrary", "arbitrary"), vmem_limit_bytes=VMEM_LIMIT),
        name="conv_ffn",
    )(x1, g1, wu_bf, wc, bc, wd_bf, g2)


def _block_diag(w):
    nblk, c, _ = w.shape
    eye = jnp.eye(nblk, dtype=w.dtype)
    return jnp.einsum("ncd,nm->ncmd", w, eye).reshape(nblk * c, nblk * c)


def kernel(x, g_pre_mix, w_in, rel_bias, w_conv_lru, b_conv_lru, w_r, b_r, w_i, b_i, lam,
           w_out, g_post_mix, g_pre_ffn, w_up, w_conv_ffn, b_conv_ffn, w_down, g_post_ffn):
    batch, seq, _ = x.shape
    depth = w_in.shape[0]
    assert seq % BLK == 0
    row = lambda a: a.reshape(1, -1).astype(F32)
    x2 = x.reshape(batch * seq, D_MODEL)
    bias = _bias_tiles(rel_bias)
    for l in range(depth):
        q, k, v, xr, gr = _inproj(x2, row(g_pre_mix[l]), w_in[l].astype(BF16))
        attn = _attention(q, k, v, bias, rel_bias, batch, seq)
        lru = _lru(xr, gr, w_conv_lru[l].astype(F32), row(b_conv_lru[l]),
                   _block_diag(w_r[l]).astype(BF16), row(b_r[l]),
                   _block_diag(w_i[l]).astype(BF16), row(b_i[l]), row(lam[l]), batch, seq)
        x2 = _outproj(x2, attn, lru, w_out[l].astype(BF16), row(g_post_mix[l]))
        x2 = _ffn(x2, row(g_pre_ffn[l]), w_up[l].astype(BF16), w_conv_ffn[l].astype(F32),
                  row(b_conv_ffn[l]), w_down[l].astype(BF16), row(g_post_ffn[l]), batch, seq)
    return x2.reshape(batch, seq, D_MODEL)
```

```python
import functools
import math

import numpy as np
import jax
import jax.numpy as jnp
from jax import lax
from jax.experimental import pallas as pl
from jax.experimental.pallas import tpu as pltpu

D_MODEL = 1024
N_HEADS = 8
HEAD_DIM = 64
D_ATTN = N_HEADS * HEAD_DIM
BLK = 256
TOPK = 3
N_BUCKETS = 32
MAX_EXACT = N_BUCKETS // 2
MAX_DISTANCE = 128
D_LRU = 512
N_LRU_BLOCKS = 8
LRU_BLOCK = D_LRU // N_LRU_BLOCKS
LRU_CONV_W = 4
LRU_C = 8.0
D_FF = 2816
FFN_CONV_W = 3
RMS_EPS = 1e-6

NEG = -1e30
ACC_ROWS = 64
VMEM_LIMIT = 56 * 1024 * 1024
F32 = jnp.float32
BF16 = jnp.bfloat16


def _rms(x, g):
    ms = jnp.mean(x * x, axis=-1, keepdims=True)
    return (x * lax.rsqrt(ms + RMS_EPS)) * g


LOG2E = math.log2(math.e)


def _gelu(x):
    k = -2.0 * math.sqrt(2.0 / math.pi) * LOG2E
    z = x * (k + (k * 0.044715) * (x * x))
    return x * (1.0 / (1.0 + jnp.exp2(z)))


def _t5_bucket_np(dist):
    n = np.maximum(dist, 0)
    nf = np.maximum(n, 1).astype(np.float64)
    val = np.log(nf / MAX_EXACT) / math.log(MAX_DISTANCE / MAX_EXACT) * (N_BUCKETS - MAX_EXACT)
    frac = np.abs(val - np.round(val))
    risky = (frac < 1e-3) & (n > MAX_EXACT) & (val < N_BUCKETS - MAX_EXACT - 0.5)
    assert not risky.any()
    large = np.minimum(MAX_EXACT + np.floor(val + 1e-6).astype(np.int64), N_BUCKETS - 1)
    return np.where(n < MAX_EXACT, n, large).astype(np.int32)


def _bucket_maps():
    k = np.arange(BLK)[:, None]
    q = np.arange(BLK)[None, :]
    adj = _t5_bucket_np(q + BLK - k)
    own = np.where(q - k >= 0, _t5_bucket_np(q - k), -1)
    return np.concatenate([adj, own], axis=0).astype(np.int32)


def _far_bucket(seq):
    far = _t5_bucket_np(np.arange(BLK + 1, max(seq, BLK + 2)))
    assert (far == far[0]).all()
    return int(far[0])


def _bias_kernel(rb_ref, bmap_ref, out_ref):
    h = pl.program_id(0)
    bmap = bmap_ref[...]
    acc = jnp.full(bmap.shape, NEG, F32)
    for b in range(N_BUCKETS):
        acc = jnp.where(bmap == b, rb_ref[b, h] * LOG2E, acc)
    out_ref[0] = acc


def _bias_tiles(rel_bias):
    bmap = jnp.asarray(_bucket_maps())
    return pl.pallas_call(
        _bias_kernel,
        grid=(N_HEADS,),
        in_specs=[
            pl.BlockSpec(memory_space=pltpu.SMEM),
            pl.BlockSpec((2 * BLK, BLK), lambda h: (0, 0)),
        ],
        out_specs=pl.BlockSpec((1, 2 * BLK, BLK), lambda h: (h, 0, 0)),
        out_shape=jax.ShapeDtypeStruct((N_HEADS, 2 * BLK, BLK), F32),
        name="bias_tiles",
    )(rel_bias.astype(F32), bmap)


def _inproj_kernel(x_ref, g_ref, w_ref, q_ref, k_ref, v_ref, xr_ref, gr_ref):
    h = _rms(x_ref[...], g_ref[...]).astype(BF16)

    def proj(lo, hi):
        return jnp.dot(h, w_ref[:, lo:hi], preferred_element_type=F32)

    q_ref[...] = (proj(0, D_ATTN) * (HEAD_DIM ** -0.5 * LOG2E)).astype(BF16)
    k_ref[...] = proj(D_ATTN, 2 * D_ATTN).astype(BF16)
    v_ref[...] = proj(2 * D_ATTN, 3 * D_ATTN).astype(BF16)
    xr_ref[...] = proj(3 * D_ATTN, 3 * D_ATTN + D_LRU)
    gr_ref[...] = proj(3 * D_ATTN + D_LRU, 3 * D_ATTN + 2 * D_LRU)


def _inproj(x2, g, w_bf, tm=512):
    n = x2.shape[0]
    d_in = w_bf.shape[1]
    tok = lambda i: (i, 0)
    return pl.pallas_call(
        _inproj_kernel,
        grid=(n // tm,),
        in_specs=[
            pl.BlockSpec((tm, D_MODEL), tok),
            pl.BlockSpec((1, D_MODEL), lambda i: (0, 0)),
            pl.BlockSpec((D_MODEL, d_in), lambda i: (0, 0)),
        ],
        out_specs=[
            pl.BlockSpec((tm, D_ATTN), tok),
            pl.BlockSpec((tm, D_ATTN), tok),
            pl.BlockSpec((tm, D_ATTN), tok),
            pl.BlockSpec((tm, D_LRU), tok),
            pl.BlockSpec((tm, D_LRU), tok),
        ],
        out_shape=[
            jax.ShapeDtypeStruct((n, D_ATTN), BF16),
            jax.ShapeDtypeStruct((n, D_ATTN), BF16),
            jax.ShapeDtypeStruct((n, D_ATTN), BF16),
            jax.ShapeDtypeStruct((n, D_LRU), F32),
            jax.ShapeDtypeStruct((n, D_LRU), F32),
        ],
        compiler_params=pltpu.CompilerParams(
            dimension_semantics=("arbitrary",), vmem_limit_bytes=VMEM_LIMIT),
        name="inproj",
    )(x2, g, w_bf)


def _attn_kernel(rb_ref, q_ref, k_ref, v_ref, bias_ref, o_ref, vt_ref, s_ref, p_ref, *, seq, far_bucket):
    nb = seq // BLK
    hp = pl.program_id(1)
    vt_ref[...] = v_ref[...].T
    lane = lax.broadcasted_iota(jnp.int32, (1, 2 * HEAD_DIM), 1)
    blk_id = lax.broadcasted_iota(jnp.int32, (nb, BLK), 0)

    k_mean = jnp.concatenate(
        [jnp.mean(k_ref[j * BLK:(j + 1) * BLK, :].astype(F32), axis=0, keepdims=True) for j in range(nb)],
        axis=0)
    km_hi = k_mean.astype(BF16)
    km_lo = (k_mean - km_hi.astype(F32)).astype(BF16)
    nt = (((1,), (1,)), ((), ()))

    def logits(i, hh):
        q_i = q_ref[i * BLK:(i + 1) * BLK, :]
        in_head = (lane >= hh * HEAD_DIM) & (lane < (hh + 1) * HEAD_DIM)
        q_h = jnp.where(in_head, q_i, jnp.zeros_like(q_i))
        c_far = rb_ref[far_bucket, 2 * hp + hh] * LOG2E
        gate = (lax.dot_general(km_hi, q_h, nt, preferred_element_type=F32)
                + lax.dot_general(km_lo, q_h, nt, preferred_element_type=F32))
        rank = jnp.zeros((nb, BLK), jnp.int32)
        for j2 in range(i):
            row = gate[j2:j2 + 1, :]
            beats = (row > gate) | ((row == gate) & (j2 < blk_id))
            rank = rank + beats.astype(jnp.int32)
        sel = rank < TOPK
        row_far = jnp.where(sel, c_far, NEG)
        row_adj = jnp.where(sel, 0.0, NEG)

        m_acc = jnp.full((ACC_ROWS, BLK), NEG, F32)
        for j in range(i + 1):
            k_j = k_ref[j * BLK:(j + 1) * BLK, :]
            s = lax.dot_general(k_j, q_h, nt, preferred_element_type=F32)
            if j == i:
                s = s + bias_ref[hh, BLK:2 * BLK, :]
            elif j == i - 1:
                s = (s + bias_ref[hh, 0:BLK, :]) + row_adj[j:j + 1, :]
            else:
                s = s + row_far[j:j + 1, :]
            s_ref[hh, j * BLK:(j + 1) * BLK, :] = s
            m_acc = jnp.maximum(m_acc, jnp.max(s.reshape(BLK // ACC_ROWS, ACC_ROWS, BLK), axis=0))
        return jnp.max(m_acc, axis=0, keepdims=True)

    def attend(i, hh, m):
        l_acc = jnp.zeros((ACC_ROWS, BLK), F32)
        for j in range(i + 1):
            p = jnp.exp2(s_ref[hh, j * BLK:(j + 1) * BLK, :] - m)
            l_acc = l_acc + jnp.sum(p.reshape(BLK // ACC_ROWS, ACC_ROWS, BLK), axis=0)
            p_ref[hh, j * BLK:(j + 1) * BLK, :] = p.astype(BF16)
        l = jnp.sum(l_acc, axis=0, keepdims=True)
        nk = (i + 1) * BLK
        o_t = jnp.dot(vt_ref[hh * HEAD_DIM:(hh + 1) * HEAD_DIM, 0:nk], p_ref[hh, 0:nk, :],
                      preferred_element_type=F32)
        return o_t / l

    items = [(i, hh) for i in range(nb) for hh in range(2)]
    m_next = logits(*items[0])
    outs = []
    for n, (i, hh) in enumerate(items):
        m_cur = m_next
        if n + 1 < len(items):
            m_next = logits(*items[n + 1])
        outs.append(attend(i, hh, m_cur))
        if hh == 1:
            o_pair = jnp.concatenate(outs, axis=0)
            o_ref[i * BLK:(i + 1) * BLK, :] = o_pair.T.astype(BF16)
            outs = []


def _attention(q, k, v, bias, rel_bias, batch, seq):
    n = q.shape[0]
    tokpair = lambda b, hp: (b, hp)
    kern = functools.partial(_attn_kernel, seq=seq, far_bucket=_far_bucket(seq))
    return pl.pallas_call(
        kern,
        grid=(batch, N_HEADS // 2),
        in_specs=[
            pl.BlockSpec(memory_space=pltpu.SMEM),
            pl.BlockSpec((seq, 2 * HEAD_DIM), tokpair),
            pl.BlockSpec((seq, 2 * HEAD_DIM), tokpair),
            pl.BlockSpec((seq, 2 * HEAD_DIM), tokpair),
            pl.BlockSpec((2, 2 * BLK, BLK), lambda b, hp: (hp, 0, 0)),
        ],
        out_specs=pl.BlockSpec((seq, 2 * HEAD_DIM), tokpair),
        out_shape=jax.ShapeDtypeStruct((n, D_ATTN), BF16),
        scratch_shapes=[
            pltpu.VMEM((2 * HEAD_DIM, seq), BF16),
            pltpu.VMEM((2, seq, BLK), F32),
            pltpu.VMEM((2, seq, BLK), BF16),
        ],
        compiler_params=pltpu.CompilerParams(
            dimension_semantics=("arbitrary", "arbitrary"), vmem_limit_bytes=VMEM_LIMIT),
        name="moba_attn",
    )(rel_bias.astype(F32), q, k, v, bias)


def _lru_kernel(xr_ref, gr_ref, wc_ref, bc_ref, wr_ref, br_ref, wi_ref, bi_ref, lam_ref, o_ref,
                xp_ref, a_ref, u_ref, h_ref, *, seq, chunk):
    pad = 8
    xp_ref[0:pad, :] = jnp.zeros((pad, D_LRU), F32)
    xp_ref[pad:pad + seq, :] = xr_ref[...]
    z = -lam_ref[...]
    softplus = jnp.maximum(z, 0.0) + jnp.log1p(jnp.exp(-jnp.abs(z)))
    c_row = -LRU_C * softplus
    sub = lax.broadcasted_iota(jnp.int32, (chunk // 8, 8, D_LRU), 1)

    carry = jnp.zeros((8, D_LRU), F32)
    for ch in range(seq // chunk):
        t0 = ch * chunk
        xc = bc_ref[...] + sum(
            wc_ref[kk:kk + 1, :] * xp_ref[t0 + pad - (LRU_CONV_W - 1) + kk:t0 + pad - (LRU_CONV_W - 1) + kk + chunk, :]
            for kk in range(LRU_CONV_W))
        xb = xc.astype(BF16)
        r = jax.nn.sigmoid(jnp.dot(xb, wr_ref[...], preferred_element_type=F32) + br_ref[...])
        ig = jax.nn.sigmoid(jnp.dot(xb, wi_ref[...], preferred_element_type=F32) + bi_ref[...])
        log_a = c_row * r
        a = jnp.exp(log_a)
        u = jnp.sqrt(jnp.tanh(-log_a) * (1.0 + a * a)) * (ig * xc)
        a3 = a.reshape(chunk // 8, 8, D_LRU)
        u3 = u.reshape(chunk // 8, 8, D_LRU)
        for d in (1, 2, 4):
            a_s = pltpu.roll(a3, d, 1)
            u_s = pltpu.roll(u3, d, 1)
            live = sub >= d
            u3 = jnp.where(live, a3 * u_s + u3, u3)
            a3 = jnp.where(live, a3 * a_s, a3)
        a_ref[...] = a3.reshape(chunk, D_LRU)
        u_ref[...] = u3.reshape(chunk, D_LRU)

        def group(gi, c):
            rows = pl.ds(pl.multiple_of(gi * 8, 8), 8)
            hg = a_ref[rows, :] * c + u_ref[rows, :]
            h_ref[rows, :] = hg
            return jnp.broadcast_to(hg[7:8, :], (8, D_LRU))

        carry = lax.fori_loop(0, chunk // 8, group, carry)
        o_ref[t0:t0 + chunk, :] = (h_ref[...] * _gelu(gr_ref[t0:t0 + chunk, :])).astype(BF16)


def _lru(xr, gr, wc, bc, wr_bd, br, wi_bd, bi, lam, batch, seq, chunk=256):
    n = xr.shape[0]
    row = lambda b: (b, 0)
    full = lambda b: (0, 0)
    kern = functools.partial(_lru_kernel, seq=seq, chunk=chunk)
    return pl.pallas_call(
        kern,
        grid=(batch,),
        in_specs=[
            pl.BlockSpec((seq, D_LRU), row),
            pl.BlockSpec((seq, D_LRU), row),
            pl.BlockSpec((LRU_CONV_W, D_LRU), full),
            pl.BlockSpec((1, D_LRU), full),
            pl.BlockSpec((D_LRU, D_LRU), full),
            pl.BlockSpec((1, D_LRU), full),
            pl.BlockSpec((D_LRU, D_LRU), full),
            pl.BlockSpec((1, D_LRU), full),
            pl.BlockSpec((1, D_LRU), full),
        ],
        out_specs=pl.BlockSpec((seq, D_LRU), row),
        out_shape=jax.ShapeDtypeStruct((n, D_LRU), BF16),
        scratch_shapes=[
            pltpu.VMEM((seq + 8, D_LRU), F32),
            pltpu.VMEM((chunk, D_LRU), F32),
            pltpu.VMEM((chunk, D_LRU), F32),
            pltpu.VMEM((chunk, D_LRU), F32),
        ],
        compiler_params=pltpu.CompilerParams(
            dimension_semantics=("arbitrary",), vmem_limit_bytes=VMEM_LIMIT),
        name="rg_lru",
    )(xr, gr, wc, bc, wr_bd, br, wi_bd, bi, lam)


def _outproj_kernel(x_ref, a_ref, l_ref, w_ref, g_ref, o_ref):
    mix = (jnp.dot(a_ref[...], w_ref[0:D_ATTN, :], preferred_element_type=F32)
           + jnp.dot(l_ref[...], w_ref[D_ATTN:D_ATTN + D_LRU, :], preferred_element_type=F32))
    o_ref[...] = x_ref[...] + _rms(mix, g_ref[...])


def _outproj(x2, attn, lru, w_bf, g, tm=512):
    n = x2.shape[0]
    tok = lambda i: (i, 0)
    return pl.pallas_call(
        _outproj_kernel,
        grid=(n // tm,),
        in_specs=[
            pl.BlockSpec((tm, D_MODEL), tok),
            pl.BlockSpec((tm, D_ATTN), tok),
            pl.BlockSpec((tm, D_LRU), tok),
            pl.BlockSpec((D_ATTN + D_LRU, D_MODEL), lambda i: (0, 0)),
            pl.BlockSpec((1, D_MODEL), lambda i: (0, 0)),
        ],
        out_specs=pl.BlockSpec((tm, D_MODEL), tok),
        out_shape=jax.ShapeDtypeStruct((n, D_MODEL), F32),
        compiler_params=pltpu.CompilerParams(
            dimension_semantics=("arbitrary",), vmem_limit_bytes=VMEM_LIMIT),
        name="outproj",
    )(x2, attn, lru, w_bf, g)


def _ffn_kernel(x_ref, g1_ref, wu_ref, wc_ref, bc_ref, wd_ref, g2_ref, o_ref,
                halo_ref, us_ref, gs_ref, act_ref, *, tm, fc):
    pad = 8

    @pl.when(pl.program_id(1) == 0)
    def _():
        halo_ref[...] = jnp.zeros(halo_ref.shape, F32)

    x = x_ref[...]
    h = _rms(x, g1_ref[...]).astype(BF16)

    n_chunks = D_FF // fc
    n_slots = us_ref.shape[0]

    def stage(c):
        for s_ref, col in ((us_ref.at[c % n_slots], c * fc), (gs_ref.at[c % n_slots], D_FF + c * fc)):
            raw = jnp.dot(h, wu_ref[:, col:col + fc], preferred_element_type=F32)
            s_ref[0:pad, :] = halo_ref[:, col:col + fc]
            s_ref[pad:pad + tm, :] = raw
            halo_ref[:, col:col + fc] = raw[tm - pad:tm, :]

    def conv(s_ref, col):
        out = bc_ref[:, col:col + fc]
        for kk in range(FFN_CONV_W):
            lo = pad - (FFN_CONV_W - 1) + kk
            out = out + wc_ref[kk:kk + 1, col:col + fc] * s_ref[lo:lo + tm, :]
        return out

    stage(0)
    for c in range(n_chunks):
        if c + 1 < n_chunks:
            stage(c + 1)
        u = conv(us_ref.at[c % n_slots], c * fc)
        g = conv(gs_ref.at[c % n_slots], D_FF + c * fc)
        act_ref[:, c * fc:(c + 1) * fc] = (_gelu(g) * u).astype(BF16)
    ffn = jnp.dot(act_ref[...], wd_ref[...], preferred_element_type=F32)
    o_ref[...] = x + _rms(ffn, g2_ref[...])


def _ffn(x1, g1, wu_bf, wc, bc, wd_bf, g2, batch, seq, tm=512, fc=256):
    n = x1.shape[0]
    nt = seq // tm
    tok = lambda b, t: (b * nt + t, 0)
    full = lambda b, t: (0, 0)
    kern = functools.partial(_ffn_kernel, tm=tm, fc=fc)
    return pl.pallas_call(
        kern,
        grid=(batch, nt),
        in_specs=[
            pl.BlockSpec((tm, D_MODEL), tok),
            pl.BlockSpec((1, D_MODEL), full),
            pl.BlockSpec((D_MODEL, 2 * D_FF), full, pipeline_mode=pl.Buffered(1)),
            pl.BlockSpec((FFN_CONV_W, 2 * D_FF), full),
            pl.BlockSpec((1, 2 * D_FF), full),
            pl.BlockSpec((D_FF, D_MODEL), full, pipeline_mode=pl.Buffered(1)),
            pl.BlockSpec((1, D_MODEL), full),
        ],
        out_specs=pl.BlockSpec((tm, D_MODEL), tok),
        out_shape=jax.ShapeDtypeStruct((n, D_MODEL), F32),
        scratch_shapes=[
            pltpu.VMEM((8, 2 * D_FF), F32),
            pltpu.VMEM((3, tm + 8, fc), F32),
            pltpu.VMEM((3, tm + 8, fc), F32),
            pltpu.VMEM((tm, D_FF), BF16),
        ],
        compiler_params=pltpu.CompilerParams(
            dimension_semantics=("arbitrary", "arbitrary"), vmem_limit_bytes=VMEM_LIMIT),
        name="conv_ffn",
    )(x1, g1, wu_bf, wc, bc, wd_bf, g2)


def _block_diag(w):
    nblk, c, _ = w.shape
    eye = jnp.eye(nblk, dtype=w.dtype)
    return jnp.einsum("ncd,nm->ncmd", w, eye).reshape(nblk * c, nblk * c)


def kernel(x, g_pre_mix, w_in, rel_bias, w_conv_lru, b_conv_lru, w_r, b_r, w_i, b_i, lam,
           w_out, g_post_mix, g_pre_ffn, w_up, w_conv_ffn, b_conv_ffn, w_down, g_post_ffn):
    batch, seq, _ = x.shape
    depth = w_in.shape[0]
    assert seq % BLK == 0
    row = lambda a: a.reshape(1, -1).astype(F32)
    x2 = x.reshape(batch * seq, D_MODEL)
    bias = _bias_tiles(rel_bias)
    for l in range(depth):
        q, k, v, xr, gr = _inproj(x2, row(g_pre_mix[l]), w_in[l].astype(BF16))
        attn = _attention(q, k, v, bias, rel_bias, batch, seq)
        lru = _lru(xr, gr, w_conv_lru[l].astype(F32), row(b_conv_lru[l]),
                   _block_diag(w_r[l]).astype(BF16), row(b_r[l]),
                   _block_diag(w_i[l]).astype(BF16), row(b_i[l]), row(lam[l]), batch, seq)
        x2 = _outproj(x2, attn, lru, w_out[l].astype(BF16), row(g_post_mix[l]))
        x2 = _ffn(x2, row(g_pre_ffn[l]), w_up[l].astype(BF16), w_conv_ffn[l].astype(F32),
                  row(b_conv_ffn[l]), w_down[l].astype(BF16), row(g_post_ffn[l]), batch, seq)
    return x2.reshape(batch, seq, D_MODEL)
```

```python
import functools
import math

import numpy as np
import jax
import jax.numpy as jnp
from jax import lax
from jax.experimental import pallas as pl
from jax.experimental.pallas import tpu as pltpu

D_MODEL = 1024
N_HEADS = 8
HEAD_DIM = 64
D_ATTN = N_HEADS * HEAD_DIM
BLK = 256
TOPK = 3
N_BUCKETS = 32
MAX_EXACT = N_BUCKETS // 2
MAX_DISTANCE = 128
D_LRU = 512
LRU_CONV_W = 4
LRU_C = 8.0
D_FF = 2816
FFN_CONV_W = 3
RMS_EPS = 1e-6

SUBLANES = 8
STRAND = BLK // SUBLANES
NEG = -1e30
ACC_ROWS = 64
VMEM_LIMIT = 56 * 1024 * 1024
LOG2E = math.log2(math.e)
F32 = jnp.float32
BF16 = jnp.bfloat16


def _rms(x, g):
    ms = jnp.mean(x * x, axis=-1, keepdims=True)
    return (x * lax.rsqrt(ms + RMS_EPS)) * g


def _gelu(x):
    k = -2.0 * math.sqrt(2.0 / math.pi) * LOG2E
    z = x * (k + (k * 0.044715) * (x * x))
    return x * (1.0 / (1.0 + jnp.exp2(z)))


def _sigmoid(z):
    return 1.0 / (1.0 + jnp.exp2(z * (-LOG2E)))


def _to_strands(x):
    return pltpu.einshape("(bsv)c->(bvs)c", x, b=x.shape[0] // BLK, s=SUBLANES)


def _from_strands(x):
    return pltpu.einshape("(bvs)c->(bsv)c", x, b=x.shape[0] // BLK, v=STRAND)


def _shift_head(prev_tail, cur_tail):
    n = cur_tail.shape[0] // SUBLANES
    cur3 = cur_tail.reshape(n, SUBLANES, cur_tail.shape[1])
    prev3 = prev_tail.reshape(n, SUBLANES, cur_tail.shape[1])
    sub = lax.broadcasted_iota(jnp.int32, cur3.shape, 1)
    head = jnp.where(sub == 0, pltpu.roll(prev3, 1, 1), pltpu.roll(cur3, 1, 1))
    return head.reshape(cur_tail.shape)


def _causal_conv(cur, prev_tail, w_ref, b_row, col, width):
    n = width - 1
    c = cur.shape[1]
    ext = jnp.concatenate([_shift_head(prev_tail, cur[BLK - n * SUBLANES:, :]), cur], axis=0)
    out = b_row
    for kk in range(width):
        out = out + w_ref[kk:kk + 1, col:col + c] * ext[kk * SUBLANES:kk * SUBLANES + BLK, :]
    return out


def _t5_bucket_np(dist):
    n = np.maximum(dist, 0)
    nf = np.maximum(n, 1).astype(np.float64)
    val = np.log(nf / MAX_EXACT) / math.log(MAX_DISTANCE / MAX_EXACT) * (N_BUCKETS - MAX_EXACT)
    frac = np.abs(val - np.round(val))
    risky = (frac < 1e-3) & (n > MAX_EXACT) & (val < N_BUCKETS - MAX_EXACT - 0.5)
    assert not risky.any()
    large = np.minimum(MAX_EXACT + np.floor(val + 1e-6).astype(np.int64), N_BUCKETS - 1)
    return np.where(n < MAX_EXACT, n, large).astype(np.int32)


def _bucket_maps():
    row = np.arange(BLK)
    pos = (row % SUBLANES) * STRAND + row // SUBLANES
    k = pos[:, None]
    q = pos[None, :]
    adj = _t5_bucket_np(q + BLK - k)
    own = np.where(q - k >= 0, _t5_bucket_np(q - k), -1)
    return np.concatenate([adj, own], axis=0).astype(np.int32)


def _far_bucket(seq):
    far = _t5_bucket_np(np.arange(BLK + 1, max(seq, BLK + 2)))
    assert (far == far[0]).all()
    return int(far[0])


def _bias_kernel(rb_ref, bmap_ref, out_ref):
    h = pl.program_id(0)
    bmap = bmap_ref[...]
    acc = jnp.full(bmap.shape, NEG, F32)
    for b in range(N_BUCKETS):
        acc = jnp.where(bmap == b, rb_ref[b, h] * LOG2E, acc)
    out_ref[0] = acc


def _bias_tiles(rel_bias):
    bmap = jnp.asarray(_bucket_maps())
    return pl.pallas_call(
        _bias_kernel,
        grid=(N_HEADS,),
        in_specs=[
            pl.BlockSpec(memory_space=pltpu.SMEM),
            pl.BlockSpec((2 * BLK, BLK), lambda h: (0, 0)),
        ],
        out_specs=pl.BlockSpec((1, 2 * BLK, BLK), lambda h: (h, 0, 0)),
        out_shape=jax.ShapeDtypeStruct((N_HEADS, 2 * BLK, BLK), F32),
        name="bias_tiles",
    )(rel_bias.astype(F32), bmap)


def _lru_block(xc, gates, gr, h_in, c_pos):
    r = _sigmoid(gates[:, 0:D_LRU])
    ig = _sigmoid(gates[:, D_LRU:2 * D_LRU])
    a = jnp.exp2(r * (c_pos * (-LOG2E)))
    u = jnp.sqrt(jnp.tanh(r * c_pos) * (1.0 + a * a)) * (ig * xc)

    a3 = a.reshape(STRAND, SUBLANES, D_LRU)
    u3 = u.reshape(STRAND, SUBLANES, D_LRU)
    h_loc = [u3[0]]
    a_cum = [a3[0]]
    for v in range(1, STRAND):
        h_loc.append(a3[v] * h_loc[-1] + u3[v])
        a_cum.append(a3[v] * a_cum[-1])
    a_e, u_e = a_cum[-1], h_loc[-1]
    sub = lax.broadcasted_iota(jnp.int32, (SUBLANES, D_LRU), 0)
    for d in (1, 2, 4):
        live = sub >= d
        u_e = jnp.where(live, a_e * pltpu.roll(u_e, d, 0) + u_e, u_e)
        a_e = jnp.where(live, a_e * pltpu.roll(a_e, d, 0), a_e)
    ends = a_e * h_in + u_e
    starts = jnp.where(sub == 0, h_in, pltpu.roll(ends, 1, 0))
    h = jnp.concatenate([h_loc[v] + a_cum[v] * starts for v in range(STRAND)], axis=0)
    out = (h * _gelu(gr)).astype(BF16)
    return out, jnp.broadcast_to(ends[SUBLANES - 1:SUBLANES, :], (SUBLANES, D_LRU))


def _inproj_kernel(x_ref, g_ref, w_ref, wc_ref, bc_ref, wg_ref, bg_ref, lam_ref,
                   q_ref, k_ref, v_ref, lru_ref, tail_ref, state_ref, *, tm):
    n_tail = (LRU_CONV_W - 1) * SUBLANES
    n_blk = tm // BLK

    @pl.when(pl.program_id(1) == 0)
    def _():
        tail_ref[...] = jnp.zeros(tail_ref.shape, F32)
        state_ref[...] = jnp.zeros(state_ref.shape, F32)

    h = _rms(_to_strands(x_ref[...]), g_ref[...]).astype(BF16)

    def proj(lo, hi):
        return jnp.dot(h, w_ref[:, lo:hi], preferred_element_type=F32)

    xr = proj(3 * D_ATTN, 3 * D_ATTN + D_LRU)
    gr = proj(3 * D_ATTN + D_LRU, 3 * D_ATTN + 2 * D_LRU)
    xc = []
    tail = tail_ref[...]
    for blk in range(n_blk):
        cur = xr[blk * BLK:(blk + 1) * BLK, :]
        xc.append(_causal_conv(cur, tail, wc_ref, bc_ref[...], 0, LRU_CONV_W))
        tail = cur[BLK - n_tail:, :]
    tail_ref[...] = tail
    xc = jnp.concatenate(xc, axis=0)
    q_ref[...] = (proj(0, D_ATTN) * (HEAD_DIM ** -0.5 * LOG2E)).astype(BF16)
    gates = jnp.dot(xc.astype(BF16), wg_ref[...], preferred_element_type=F32) + bg_ref[...]
    k_ref[...] = proj(D_ATTN, 2 * D_ATTN).astype(BF16)
    v_ref[...] = proj(2 * D_ATTN, 3 * D_ATTN).astype(BF16)

    z = -lam_ref[...]
    c_pos = LRU_C * (jnp.maximum(z, 0.0) + jnp.log1p(jnp.exp(-jnp.abs(z))))
    state = state_ref[...]
    for blk in range(n_blk):
        rows = slice(blk * BLK, (blk + 1) * BLK)
        out, state = _lru_block(xc[rows, :], gates[rows, :], gr[rows, :], state, c_pos)
        lru_ref[rows, :] = out
    state_ref[...] = state


def _inproj(x2, g, w_bf, wc, bc, wg_bf, bg, lam, batch, seq, tm=512):
    n = x2.shape[0]
    nt = seq // tm
    d_in = w_bf.shape[1]
    tok = lambda b, t: (b * nt + t, 0)
    full = lambda b, t: (0, 0)
    once = dict(pipeline_mode=pl.Buffered(1))
    kern = functools.partial(_inproj_kernel, tm=tm)
    return pl.pallas_call(
        kern,
        grid=(batch, nt),
        in_specs=[
            pl.BlockSpec((tm, D_MODEL), tok),
            pl.BlockSpec((1, D_MODEL), full),
            pl.BlockSpec((D_MODEL, d_in), full, **once),
            pl.BlockSpec((LRU_CONV_W, D_LRU), full),
            pl.BlockSpec((1, D_LRU), full),
            pl.BlockSpec((D_LRU, 2 * D_LRU), full, **once),
            pl.BlockSpec((1, 2 * D_LRU), full),
            pl.BlockSpec((1, D_LRU), full),
        ],
        out_specs=[pl.BlockSpec((tm, D_ATTN), tok)] * 3 + [pl.BlockSpec((tm, D_LRU), tok)],
        out_shape=[jax.ShapeDtypeStruct((n, D_ATTN), BF16)] * 3 + [jax.ShapeDtypeStruct((n, D_LRU), BF16)],
        scratch_shapes=[
            pltpu.VMEM(((LRU_CONV_W - 1) * SUBLANES, D_LRU), F32),
            pltpu.VMEM((SUBLANES, D_LRU), F32),
        ],
        compiler_params=pltpu.CompilerParams(
            dimension_semantics=("arbitrary", "arbitrary"), vmem_limit_bytes=VMEM_LIMIT),
        name="inproj_lru",
    )(x2, g, w_bf, wc, bc, wg_bf, bg, lam)


def _attn_kernel(rb_ref, q_ref, k_ref, v_ref, bias_ref, o_ref, vt_ref, s_ref, p_ref, *, seq, far_bucket):
    nb = seq // BLK
    hp = pl.program_id(1)
    vt_ref[...] = v_ref[...].T
    lane = lax.broadcasted_iota(jnp.int32, (1, 2 * HEAD_DIM), 1)
    blk_id = lax.broadcasted_iota(jnp.int32, (nb, BLK), 0)

    k_mean = jnp.concatenate(
        [jnp.mean(k_ref[j * BLK:(j + 1) * BLK, :].astype(F32), axis=0, keepdims=True) for j in range(nb)],
        axis=0)
    km_hi = k_mean.astype(BF16)
    km_lo = (k_mean - km_hi.astype(F32)).astype(BF16)
    nt = (((1,), (1,)), ((), ()))

    def logits(i, hh):
        q_i = q_ref[i * BLK:(i + 1) * BLK, :]
        in_head = (lane >= hh * HEAD_DIM) & (lane < (hh + 1) * HEAD_DIM)
        q_h = jnp.where(in_head, q_i, jnp.zeros_like(q_i))
        c_far = rb_ref[far_bucket, 2 * hp + hh] * LOG2E
        gate = (lax.dot_general(km_hi, q_h, nt, preferred_element_type=F32)
                + lax.dot_general(km_lo, q_h, nt, preferred_element_type=F32))
        rank = jnp.zeros((nb, BLK), jnp.int32)
        for j2 in range(i):
            row = gate[j2:j2 + 1, :]
            beats = (row > gate) | ((row == gate) & (j2 < blk_id))
            rank = rank + beats.astype(jnp.int32)
        sel = rank < TOPK
        row_far = jnp.where(sel, c_far, NEG)
        row_adj = jnp.where(sel, 0.0, NEG)

        m_acc = jnp.full((ACC_ROWS, BLK), NEG, F32)
        for j in range(i + 1):
            k_j = k_ref[j * BLK:(j + 1) * BLK, :]
            s = lax.dot_general(k_j, q_h, nt, preferred_element_type=F32)
            if j == i:
                s = s + bias_ref[hh, BLK:2 * BLK, :]
            elif j == i - 1:
                s = (s + bias_ref[hh, 0:BLK, :]) + row_adj[j:j + 1, :]
            else:
                s = s + row_far[j:j + 1, :]
            s_ref[hh, j * BLK:(j + 1) * BLK, :] = s
            m_acc = jnp.maximum(m_acc, jnp.max(s.reshape(BLK // ACC_ROWS, ACC_ROWS, BLK), axis=0))
        return jnp.max(m_acc, axis=0, keepdims=True)

    def attend(i, hh, m):
        l_acc = jnp.zeros((ACC_ROWS, BLK), F32)
        for j in range(i + 1):
            p = jnp.exp2(s_ref[hh, j * BLK:(j + 1) * BLK, :] - m)
            l_acc = l_acc + jnp.sum(p.reshape(BLK // ACC_ROWS, ACC_ROWS, BLK), axis=0)
            p_ref[hh, j * BLK:(j + 1) * BLK, :] = p.astype(BF16)
        l = jnp.sum(l_acc, axis=0, keepdims=True)
        nk = (i + 1) * BLK
        o_t = jnp.dot(vt_ref[hh * HEAD_DIM:(hh + 1) * HEAD_DIM, 0:nk], p_ref[hh, 0:nk, :],
                      preferred_element_type=F32)
        return o_t / l

    items = [(i, hh) for i in range(nb) for hh in range(2)]
    m_next = logits(*items[0])
    outs = []
    for n, (i, hh) in enumerate(items):
        m_cur = m_next
        if n + 1 < len(items):
            m_next = logits(*items[n + 1])
        outs.append(attend(i, hh, m_cur))
        if hh == 1:
            o_pair = jnp.concatenate(outs, axis=0)
            o_ref[i * BLK:(i + 1) * BLK, :] = o_pair.T.astype(BF16)
            outs = []


def _attention(q, k, v, bias, rel_bias, batch, seq):
    n = q.shape[0]
    tokpair = lambda b, hp: (b, hp)
    kern = functools.partial(_attn_kernel, seq=seq, far_bucket=_far_bucket(seq))
    return pl.pallas_call(
        kern,
        grid=(batch, N_HEADS // 2),
        in_specs=[
            pl.BlockSpec(memory_space=pltpu.SMEM),
            pl.BlockSpec((seq, 2 * HEAD_DIM), tokpair),
            pl.BlockSpec((seq, 2 * HEAD_DIM), tokpair),
            pl.BlockSpec((seq, 2 * HEAD_DIM), tokpair),
            pl.BlockSpec((2, 2 * BLK, BLK), lambda b, hp: (hp, 0, 0)),
        ],
        out_specs=pl.BlockSpec((seq, 2 * HEAD_DIM), tokpair),
        out_shape=jax.ShapeDtypeStruct((n, D_ATTN), BF16),
        scratch_shapes=[
            pltpu.VMEM((2 * HEAD_DIM, seq), BF16),
            pltpu.VMEM((2, seq, BLK), F32),
            pltpu.VMEM((2, seq, BLK), BF16),
        ],
        compiler_params=pltpu.CompilerParams(
            dimension_semantics=("arbitrary", "arbitrary"), vmem_limit_bytes=VMEM_LIMIT),
        name="moba_attn",
    )(rel_bias.astype(F32), q, k, v, bias)


def _ffn_kernel(x_ref, a_ref, l_ref, wo_ref, g0_ref, g1_ref, wu_ref, wc_ref, bc_ref, wd_ref, g2_ref,
                o_ref, halo_ref, act_ref, *, tm, fc):
    n_tail = (FFN_CONV_W - 1) * SUBLANES
    n_blk = tm // BLK

    @pl.when(pl.program_id(1) == 0)
    def _():
        halo_ref[...] = jnp.zeros(halo_ref.shape, F32)

    mix = (jnp.dot(a_ref[...], wo_ref[0:D_ATTN, :], preferred_element_type=F32)
           + jnp.dot(l_ref[...], wo_ref[D_ATTN:D_ATTN + D_LRU, :], preferred_element_type=F32))
    x1 = _to_strands(x_ref[...]) + _rms(mix, g0_ref[...])
    h = _rms(x1, g1_ref[...]).astype(BF16)

    n_chunks = D_FF // fc

    def up(c):
        return [jnp.dot(h, wu_ref[:, col:col + fc], preferred_element_type=F32)
                for col in (c * fc, D_FF + c * fc)]

    def conv(raw, col):
        outs = []
        tail = halo_ref[:, col:col + fc]
        for blk in range(n_blk):
            cur = raw[blk * BLK:(blk + 1) * BLK, :]
            outs.append(_causal_conv(cur, tail, wc_ref, bc_ref[:, col:col + fc], col, FFN_CONV_W))
            tail = cur[BLK - n_tail:, :]
        halo_ref[:, col:col + fc] = tail
        return jnp.concatenate(outs, axis=0)

    nxt = up(0)
    for c in range(n_chunks):
        raw_u, raw_g = nxt
        if c + 1 < n_chunks:
            nxt = up(c + 1)
        u = conv(raw_u, c * fc)
        g = conv(raw_g, D_FF + c * fc)
        act_ref[:, c * fc:(c + 1) * fc] = (_gelu(g) * u).astype(BF16)
    ffn = jnp.dot(act_ref[...], wd_ref[...], preferred_element_type=F32)
    o_ref[...] = _from_strands(x1 + _rms(ffn, g2_ref[...]))


def _ffn(x2, attn, lru, wo_bf, g0, g1, wu_bf, wc, bc, wd_bf, g2, batch, seq, tm=512, fc=256):
    n = x2.shape[0]
    nt = seq // tm
    tok = lambda b, t: (b * nt + t, 0)
    full = lambda b, t: (0, 0)
    once = dict(pipeline_mode=pl.Buffered(1))
    kern = functools.partial(_ffn_kernel, tm=tm, fc=fc)
    return pl.pallas_call(
        kern,
        grid=(batch, nt),
        in_specs=[
            pl.BlockSpec((tm, D_MODEL), tok),
            pl.BlockSpec((tm, D_ATTN), tok),
            pl.BlockSpec((tm, D_LRU), tok),
            pl.BlockSpec((D_ATTN + D_LRU, D_MODEL), full, **once),
            pl.BlockSpec((1, D_MODEL), full),
            pl.BlockSpec((1, D_MODEL), full),
            pl.BlockSpec((D_MODEL, 2 * D_FF), full, **once),
            pl.BlockSpec((FFN_CONV_W, 2 * D_FF), full),
            pl.BlockSpec((1, 2 * D_FF), full),
            pl.BlockSpec((D_FF, D_MODEL), full, **once),
            pl.BlockSpec((1, D_MODEL), full),
        ],
        out_specs=pl.BlockSpec((tm, D_MODEL), tok),
        out_shape=jax.ShapeDtypeStruct((n, D_MODEL), F32),
        scratch_shapes=[
            pltpu.VMEM(((FFN_CONV_W - 1) * SUBLANES, 2 * D_FF), F32),
            pltpu.VMEM((tm, D_FF), BF16),
        ],
        compiler_params=pltpu.CompilerParams(
            dimension_semantics=("arbitrary", "arbitrary"), vmem_limit_bytes=VMEM_LIMIT),
        name="outproj_ffn",
    )(x2, attn, lru, wo_bf, g0, g1, wu_bf, wc, bc, wd_bf, g2)


def _block_diag(w):
    nblk, c, _ = w.shape
    eye = jnp.eye(nblk, dtype=w.dtype)
    return jnp.einsum("ncd,nm->ncmd", w, eye).reshape(nblk * c, nblk * c)


def kernel(x, g_pre_mix, w_in, rel_bias, w_conv_lru, b_conv_lru, w_r, b_r, w_i, b_i, lam,
           w_out, g_post_mix, g_pre_ffn, w_up, w_conv_ffn, b_conv_ffn, w_down, g_post_ffn):
    batch, seq, _ = x.shape
    depth = w_in.shape[0]
    assert seq % BLK == 0
    row = lambda a: a.reshape(1, -1).astype(F32)
    x2 = x.reshape(batch * seq, D_MODEL)
    bias = _bias_tiles(rel_bias)
    for l in range(depth):
        w_gates = jnp.concatenate([_block_diag(w_r[l]), _block_diag(w_i[l])], axis=1).astype(BF16)
        b_gates = jnp.concatenate([row(b_r[l]), row(b_i[l])], axis=1)
        q, k, v, lru = _inproj(x2, row(g_pre_mix[l]), w_in[l].astype(BF16), w_conv_lru[l].astype(F32),
                               row(b_conv_lru[l]), w_gates, b_gates, row(lam[l]), batch, seq)
        attn = _attention(q, k, v, bias, rel_bias, batch, seq)
        x2 = _ffn(x2, attn, lru, w_out[l].astype(BF16), row(g_post_mix[l]), row(g_pre_ffn[l]),
                  w_up[l].astype(BF16), w_conv_ffn[l].astype(F32), row(b_conv_ffn[l]),
                  w_down[l].astype(BF16), row(g_post_ffn[l]), batch, seq)
    return x2.reshape(batch, seq, D_MODEL)
```
